```python
import jax, jax.numpy as jnp
from jax import lax
import numpy as np

D_MODEL = 4096
BATCH = 4
SEQ = 4096
DEPTH = 2
DEC_BATCH = 32
DEC_SEQ = 16
PAST_LEN = 2048

CHUNK = 64
N_A = DEPTH // 2
N_B = DEPTH - N_A
HEAD_DIM = 128
H_A = D_MODEL // HEAD_DIM
H_B = D_MODEL // HEAD_DIM
Q_BLOCK = 128
LEFT_CHUNKS = 8
B_REACH = LEFT_CHUNKS * CHUNK
B_BAND = (LEFT_CHUNKS + 1) * CHUNK
REL_CLIP = 128
N_REL = 2 * REL_CLIP + 1
FFN_DIM = ((8 * D_MODEL + 3 * 256 - 1) // (3 * 256)) * 256
ALPHA = (2 * DEPTH) ** 0.25
BETA = (8 * DEPTH) ** -0.25
LN_EPS = 1e-5
NEG_INF = -1e30

kernel_name = "yoco_stickbreak_chunkband_encoder_step"


def layer_norm(x, g, b):
    xf = x.astype(jnp.float32)
    mu = jnp.mean(xf, axis=-1, keepdims=True)
    var = jnp.mean(jnp.square(xf - mu), axis=-1, keepdims=True)
    y = (xf - mu) * lax.rsqrt(var + LN_EPS) * g.astype(jnp.float32) + b.astype(jnp.float32)
    return y.astype(x.dtype)


def swiglu(x, w_in, w_out):
    gate, up = jnp.split(x @ w_in, 2, axis=-1)
    return (jax.nn.silu(gate) * up) @ w_out


def split_heads(t, n):
    return t.reshape(t.shape[:-1] + (n, HEAD_DIM))


def merge_heads(t):
    return t.reshape(t.shape[:2] + (D_MODEL,))


def stick_breaking(q, k, v, q_pos, k_pos):
    z = jnp.einsum("bqhd,bkhd->bhqk", q, k).astype(jnp.float32) * HEAD_DIM ** -0.5
    causal = k_pos[None, :] < q_pos[:, None]
    log_beta = jax.nn.log_sigmoid(z)
    log_keep = jnp.where(causal, jax.nn.log_sigmoid(-z), 0.0)
    after = lax.cumsum(log_keep, axis=3, reverse=True) - log_keep
    w = jnp.where(causal, jnp.exp(log_beta + after), 0.0)
    return jnp.einsum("bhqk,bkhd->bqhd", w.astype(v.dtype), v)


def stick_breaking_prompt(q, k, v):
    b, s = q.shape[:2]
    nb = s // Q_BLOCK
    k_pos = jnp.arange(s)
    qb = q.reshape(b, nb, Q_BLOCK, H_A, HEAD_DIM).swapaxes(0, 1)

    def block(args):
        q_blk, i = args
        q_pos = i * Q_BLOCK + jnp.arange(Q_BLOCK)
        return stick_breaking(q_blk, k, v, q_pos, k_pos)

    o = lax.map(block, (qb, jnp.arange(nb)))
    return o.swapaxes(0, 1).reshape(b, s, H_A, HEAD_DIM)


def band_attention(q, k, v, q_pos, k_pos, rel_bias):
    s = jnp.einsum("bqhd,bkhd->bhqk", q, k).astype(jnp.float32) * HEAD_DIM ** -0.5
    rel = jnp.clip(q_pos[:, None] - k_pos[None, :], -REL_CLIP, REL_CLIP) + REL_CLIP
    s = s + rel_bias.astype(jnp.float32)[:, rel]
    qc = q_pos[:, None] // CHUNK
    kc = k_pos[None, :] // CHUNK
    ok = (k_pos[None, :] >= 0) & (kc <= qc) & (kc >= qc - LEFT_CHUNKS)
    p = jax.nn.softmax(jnp.where(ok, s, NEG_INF), axis=-1)
    return jnp.einsum("bhqk,bkhd->bqhd", p.astype(v.dtype), v)


def band_prompt(q, k_pad, v_pad, rel_bias):
    b, s = q.shape[:2]
    nc = s // CHUNK
    qc = q.reshape(b, nc, CHUNK, H_B, HEAD_DIM).swapaxes(0, 1)

    def chunk(args):
        q_c, c = args
        start = c * CHUNK
        k_band = lax.dynamic_slice_in_dim(k_pad, start, B_BAND, axis=1)
        v_band = lax.dynamic_slice_in_dim(v_pad, start, B_BAND, axis=1)
        q_pos = start + jnp.arange(CHUNK)
        k_pos = start - B_REACH + jnp.arange(B_BAND)
        return band_attention(q_c, k_band, v_band, q_pos, k_pos, rel_bias)

    o = lax.map(chunk, (qc, jnp.arange(nc)))
    return o.swapaxes(0, 1).reshape(b, s, H_B, HEAD_DIM)


def setup_inputs(seed: int = 0) -> dict:
    key = jax.random.key(seed)
    ks = jax.random.split(key, 20)
    f32 = jnp.float32

    def nrm(k, shape, scale):
        return jax.random.normal(k, shape, f32) * scale

    d_inv = D_MODEL ** -0.5
    b_len = min(B_REACH, PAST_LEN)
    a_w_qkv = jnp.concatenate([nrm(ks[6], (N_A, D_MODEL, 2 * D_MODEL), d_inv),
                               nrm(ks[7], (N_A, D_MODEL, D_MODEL), d_inv * BETA)], axis=-1)
    kv_w = jnp.concatenate([nrm(ks[9], (D_MODEL, D_MODEL), d_inv),
                            nrm(ks[10], (D_MODEL, D_MODEL), d_inv * BETA)], axis=-1)
    return {
        "x_prompt": nrm(ks[0], (BATCH, SEQ, D_MODEL), 1.0),
        "x_sample": nrm(ks[1], (DEC_BATCH, DEC_SEQ, D_MODEL), 1.0),
        "cache_a_k": nrm(ks[2], (N_A, DEC_BATCH, PAST_LEN, H_A, HEAD_DIM), 1.0),
        "cache_a_v": nrm(ks[3], (N_A, DEC_BATCH, PAST_LEN, H_A, HEAD_DIM), BETA),
        "cache_b_k": nrm(ks[4], (DEC_BATCH, b_len, H_B, HEAD_DIM), 1.0),
        "cache_b_v": nrm(ks[5], (DEC_BATCH, b_len, H_B, HEAD_DIM), BETA),
        "a_w_qkv": a_w_qkv,
        "a_w_o": nrm(ks[8], (N_A, D_MODEL, D_MODEL), d_inv * BETA),
        "kv_w": kv_w,
        "b_w_q": nrm(ks[11], (N_B, D_MODEL, D_MODEL), d_inv),
        "b_w_o": nrm(ks[12], (N_B, D_MODEL, D_MODEL), d_inv * BETA),
        "b_rel_bias": nrm(ks[13], (N_B, H_B, N_REL), 0.1),
        "ffn_w_in": nrm(ks[14], (DEPTH, D_MODEL, 2 * FFN_DIM), d_inv * BETA),
        "ffn_w_out": nrm(ks[15], (DEPTH, FFN_DIM, D_MODEL), FFN_DIM ** -0.5 * BETA),
        "ln_g": 1.0 + nrm(ks[16], (DEPTH, 2, D_MODEL), 0.02),
        "ln_b": nrm(ks[17], (DEPTH, 2, D_MODEL), 0.02),
    }


def reference(x_prompt, x_sample, cache_a_k, cache_a_v, cache_b_k, cache_b_v,
              a_w_qkv, a_w_o, kv_w, b_w_q, b_w_o, b_rel_bias, ffn_w_in, ffn_w_out, ln_g, ln_b):
    past = cache_a_k.shape[2]
    b_len = cache_b_k.shape[1]
    s = x_prompt.shape[1]
    t = x_sample.shape[1]
    q_pos_s = past + jnp.arange(t)
    xp, xs = x_prompt, x_sample
    a_kp, a_vp, a_ks, a_vs = [], [], [], []
    for l in range(DEPTH):
        if l < N_A:
            qp, kp, vp = [split_heads(u, H_A) for u in jnp.split(xp @ a_w_qkv[l], 3, axis=-1)]
            qs, k_new, v_new = [split_heads(u, H_A) for u in jnp.split(xs @ a_w_qkv[l], 3, axis=-1)]
            mp = stick_breaking_prompt(qp, kp, vp)
            k_all = jnp.concatenate([cache_a_k[l], k_new], axis=1)
            v_all = jnp.concatenate([cache_a_v[l], v_new], axis=1)
            ms = stick_breaking(qs, k_all, v_all, q_pos_s, jnp.arange(past + t))
            a_kp.append(kp)
            a_vp.append(vp)
            a_ks.append(k_new)
            a_vs.append(v_new)
            mp = merge_heads(mp) @ a_w_o[l]
            ms = merge_heads(ms) @ a_w_o[l]
        else:
            j = l - N_A
            if j == 0:
                kbp, vbp = [split_heads(u, H_B) for u in jnp.split(xp @ kv_w, 2, axis=-1)]
                kbs, vbs = [split_heads(u, H_B) for u in jnp.split(xs @ kv_w, 2, axis=-1)]
                pad = ((0, 0), (B_REACH, 0), (0, 0), (0, 0))
                kbp_pad = jnp.pad(kbp, pad)
                vbp_pad = jnp.pad(vbp, pad)
                kb_all = jnp.concatenate([cache_b_k, kbs], axis=1)
                vb_all = jnp.concatenate([cache_b_v, vbs], axis=1)
                kb_pos = jnp.concatenate([past - b_len + jnp.arange(b_len), q_pos_s])
            qp = split_heads(xp @ b_w_q[j], H_B)
            qs = split_heads(xs @ b_w_q[j], H_B)
            mp = band_prompt(qp, kbp_pad, vbp_pad, b_rel_bias[j])
            ms = band_attention(qs, kb_all, vb_all, q_pos_s, kb_pos, b_rel_bias[j])
            mp = merge_heads(mp) @ b_w_o[j]
            ms = merge_heads(ms) @ b_w_o[j]
        xp = layer_norm(ALPHA * xp + mp, ln_g[l, 0], ln_b[l, 0])
        xs = layer_norm(ALPHA * xs + ms, ln_g[l, 0], ln_b[l, 0])
        xp = layer_norm(ALPHA * xp + swiglu(xp, ffn_w_in[l], ffn_w_out[l]), ln_g[l, 1], ln_b[l, 1])
        xs = layer_norm(ALPHA * xs + swiglu(xs, ffn_w_in[l], ffn_w_out[l]), ln_g[l, 1], ln_b[l, 1])
    keep = min(B_REACH, s)
    return (xp, xs, jnp.stack(a_kp), jnp.stack(a_vp), kbp[:, s - keep:], vbp[:, s - keep:],
            jnp.stack(a_ks), jnp.stack(a_vs), kbs, vbs)
```

```python
import functools

import numpy as np
import jax
import jax.numpy as jnp
from jax import lax
from jax.experimental import pallas as pl
from jax.experimental.pallas import tpu as pltpu

HEAD_DIM = 128
CHUNK = 64
LEFT_CHUNKS = 8
B_REACH = LEFT_CHUNKS * CHUNK
B_BAND = B_REACH + CHUNK
REL_CLIP = 128
LN_EPS = 1e-5
NEG_INF = -1e30
SB_BLOCK = 128
V7X_VMEM_BYTES = 64 * 1024 * 1024
VMEM_LIMIT = V7X_VMEM_BYTES - 8 * 1024 * 1024

F32 = jnp.float32
BF16 = jnp.bfloat16


def _params(*sem):
    return pltpu.CompilerParams(dimension_semantics=sem, vmem_limit_bytes=VMEM_LIMIT)


def _mm_kernel(x_ref, w_ref, *o_refs):
    acc = jnp.dot(x_ref[...], w_ref[...], preferred_element_type=F32)
    for o_ref in o_refs:
        o_ref[...] = acc.astype(o_ref.dtype)


def matmul(x, w, out_dtypes, *, col0=0, ncols=None, bm=1024, bn=1024):
    m, k = x.shape
    ncols = w.shape[1] if ncols is None else ncols
    bm = min(bm, m)
    bn = min(bn, ncols)
    assert m % bm == 0 and ncols % bn == 0 and col0 % bn == 0
    off = col0 // bn
    outs = pl.pallas_call(
        _mm_kernel,
        grid=(m // bm, ncols // bn),
        in_specs=[pl.BlockSpec((bm, k), lambda i, j: (i, 0)),
                  pl.BlockSpec((k, bn), lambda i, j: (0, j + off))],
        out_specs=[pl.BlockSpec((bm, bn), lambda i, j: (i, j)) for _ in out_dtypes],
        out_shape=[jax.ShapeDtypeStruct((m, ncols), dt) for dt in out_dtypes],
        compiler_params=_params("parallel", "parallel"),
    )(x, w)
    return outs


def _mm_acc_kernel(x_ref, w_ref, o_ref, acc_ref):
    kk = pl.program_id(2)
    part = jnp.dot(x_ref[...], w_ref[...], preferred_element_type=F32)

    @pl.when(kk == 0)
    def _():
        acc_ref[...] = part

    @pl.when(jnp.logical_and(kk > 0, kk < pl.num_programs(2) - 1))
    def _():
        acc_ref[...] += part

    @pl.when(kk == pl.num_programs(2) - 1)
    def _():
        o_ref[...] = acc_ref[...] + part


def matmul_ksplit(x, w, *, bm=1024, bn=512, nk=2):
    m, k = x.shape
    n = w.shape[1]
    bm = min(bm, m)
    bn = min(bn, n)
    bk = k // nk
    assert m % bm == 0 and n % bn == 0 and k % nk == 0 and bk % 128 == 0 and nk >= 2
    return pl.pallas_call(
        _mm_acc_kernel,
        grid=(m // bm, n // bn, nk),
        in_specs=[pl.BlockSpec((bm, bk), lambda i, j, kk: (i, kk)),
                  pl.BlockSpec((bk, bn), lambda i, j, kk: (kk, j))],
        out_specs=pl.BlockSpec((bm, bn), lambda i, j, kk: (i, j)),
        out_shape=jax.ShapeDtypeStruct((m, n), F32),
        scratch_shapes=[pltpu.VMEM((bm, bn), F32)],
        compiler_params=_params("parallel", "parallel", "arbitrary"),
    )(x, w)


def _swiglu_kernel(x_ref, wg_ref, wu_ref, o_ref):
    x = x_ref[...]
    gate = jnp.dot(x, wg_ref[...], preferred_element_type=F32)
    up = jnp.dot(x, wu_ref[...], preferred_element_type=F32)
    o_ref[...] = (gate / (1.0 + jnp.exp(-gate)) * up).astype(o_ref.dtype)


def swiglu_in(x, w_in, *, bm=1024, bn=256):
    m, k = x.shape
    f = w_in.shape[1] // 2
    bm = min(bm, m)
    assert m % bm == 0 and f % bn == 0
    nj = f // bn
    return pl.pallas_call(
        _swiglu_kernel,
        grid=(m // bm, nj),
        in_specs=[pl.BlockSpec((bm, k), lambda i, j: (i, 0)),
                  pl.BlockSpec((k, bn), lambda i, j: (0, j)),
                  pl.BlockSpec((k, bn), lambda i, j: (0, j + nj))],
        out_specs=pl.BlockSpec((bm, bn), lambda i, j: (i, j)),
        out_shape=jax.ShapeDtypeStruct((m, f), BF16),
        compiler_params=_params("parallel", "parallel"),
    )(x, w_in, w_in)


def _ln_kernel(x_ref, m_ref, g_ref, b_ref, o_ref, ob_ref, *, alpha):
    y = alpha * x_ref[...] + m_ref[...]
    mu = jnp.mean(y, axis=-1, keepdims=True)
    yc = y - mu
    var = jnp.mean(yc * yc, axis=-1, keepdims=True)
    out = yc * lax.rsqrt(var + LN_EPS) * g_ref[...] + b_ref[...]
    o_ref[...] = out
    ob_ref[...] = out.astype(ob_ref.dtype)


def residual_ln(x, branch, g, b, alpha, *, bm=256):
    m, d = x.shape
    bm = min(bm, m)
    assert m % bm == 0
    row = pl.BlockSpec((bm, d), lambda i: (i, 0))
    vec = pl.BlockSpec((1, d), lambda i: (0, 0))
    return pl.pallas_call(
        functools.partial(_ln_kernel, alpha=alpha),
        grid=(m // bm,),
        in_specs=[row, row, vec, vec],
        out_specs=[row, row],
        out_shape=[jax.ShapeDtypeStruct((m, d), F32), jax.ShapeDtypeStruct((m, d), BF16)],
        compiler_params=_params("parallel"),
    )(x, branch, g.reshape(1, d), b.reshape(1, d))


def _suffix_sum_matrix(bk):
    r = lax.broadcasted_iota(jnp.int32, (bk, bk + 128), 0)
    c = lax.broadcasted_iota(jnp.int32, (bk, bk + 128), 1)
    return jnp.where(jnp.logical_or(r > c, c >= bk), 1.0, 0.0).astype(BF16)


def _sb_tile(q, k, v, carry, acc, suffix, causal, scale):
    bk = k.shape[0]
    z = lax.dot_general(q, k, (((1,), (1,)), ((), ())), preferred_element_type=F32) * scale
    log_beta = jnp.minimum(z, 0.0) - jnp.log(1.0 + jnp.exp(-jnp.abs(z)))
    log_keep = log_beta - z
    if causal is not None:
        log_keep = jnp.where(causal, log_keep, 0.0)
    hi = log_keep.astype(BF16)
    lo = (log_keep - hi.astype(F32)).astype(BF16)
    sums = (jnp.dot(hi, suffix, preferred_element_type=F32)
            + jnp.dot(lo, suffix, preferred_element_type=F32))
    after = sums[:, :bk]
    total = sums[:, bk:]
    if bk == 128:
        carry_b = carry
    else:
        carry_b = jnp.concatenate([carry] * (bk // 128), axis=1)
    w = jnp.exp(log_beta + after + carry_b)
    if causal is not None:
        w = jnp.where(causal, w, 0.0)
    acc = acc + jnp.dot(w.astype(BF16), v, preferred_element_type=F32)
    return carry + total, acc


def _sb_prompt_kernel(q_ref, k_ref, v_ref, o_ref, *, scale):
    blk = SB_BLOCK
    nq = q_ref.shape[0] // blk
    suffix = _suffix_sum_matrix(blk)
    row = lax.broadcasted_iota(jnp.int32, (blk, blk), 0)
    col = lax.broadcasted_iota(jnp.int32, (blk, blk), 1)
    diag_causal = col < row

    def q_block(qi, _):
        q0 = pl.multiple_of(qi * blk, blk)
        q = q_ref[pl.ds(q0, blk), :]
        zero = jnp.zeros((blk, HEAD_DIM), F32)
        carry, acc = _sb_tile(q, k_ref[pl.ds(q0, blk), :], v_ref[pl.ds(q0, blk), :],
                              zero, zero, suffix, diag_causal, scale)

        def k_block(t, state):
            k0 = pl.multiple_of((qi - 1 - t) * blk, blk)
            return _sb_tile(q, k_ref[pl.ds(k0, blk), :], v_ref[pl.ds(k0, blk), :],
                            state[0], state[1], suffix, None, scale)

        carry, acc = lax.fori_loop(0, qi, k_block, (carry, acc))
        o_ref[pl.ds(q0, blk), :] = acc.astype(o_ref.dtype)
        return 0

    lax.fori_loop(0, nq, q_block, 0)


def stick_breaking_prompt(q, k, v, batch, seq):
    n_heads = q.shape[1] // HEAD_DIM
    assert seq % SB_BLOCK == 0
    spec = pl.BlockSpec((seq, HEAD_DIM), lambda b, h: (b, h))
    return pl.pallas_call(
        functools.partial(_sb_prompt_kernel, scale=HEAD_DIM ** -0.5),
        grid=(batch, n_heads),
        in_specs=[spec, spec, spec],
        out_specs=spec,
        out_shape=jax.ShapeDtypeStruct(q.shape, BF16),
        compiler_params=_params("parallel", "parallel"),
    )(q, k, v)


def _sb_sample_kernel(q_ref, kn_ref, vn_ref, kc_ref, vc_ref, o_ref, kpad_ref, vpad_ref, *, scale):
    blk = SB_BLOCK
    t = q_ref.shape[0]
    past = kc_ref.shape[0]
    suffix = _suffix_sum_matrix(blk)
    q = q_ref[...]
    kpad_ref[...] = jnp.zeros_like(kpad_ref)
    vpad_ref[...] = jnp.zeros_like(vpad_ref)
    kpad_ref[0:t, :] = kn_ref[...]
    vpad_ref[0:t, :] = vn_ref[...]
    row = lax.broadcasted_iota(jnp.int32, (t, blk), 0)
    col = lax.broadcasted_iota(jnp.int32, (t, blk), 1)
    zero = jnp.zeros((t, HEAD_DIM), F32)
    carry, acc = _sb_tile(q, kpad_ref[...], vpad_ref[...], zero, zero, suffix, col < row, scale)
    for kb in reversed(range(past // blk)):
        k = kc_ref[kb * blk:(kb + 1) * blk, :].astype(BF16)
        v = vc_ref[kb * blk:(kb + 1) * blk, :].astype(BF16)
        carry, acc = _sb_tile(q, k, v, carry, acc, suffix, None, scale)
    o_ref[...] = acc.astype(o_ref.dtype)


def stick_breaking_sample(q, k_new, v_new, cache_k, cache_v, batch, t):
    n_heads = q.shape[1] // HEAD_DIM
    past = cache_k.shape[1]
    assert past % SB_BLOCK == 0 and t <= SB_BLOCK and t % 16 == 0
    new = pl.BlockSpec((t, HEAD_DIM), lambda b, h: (b, h))
    old = pl.BlockSpec((None, past, HEAD_DIM), lambda b, h: (b, 0, h))
    return pl.pallas_call(
        functools.partial(_sb_sample_kernel, scale=HEAD_DIM ** -0.5),
        grid=(batch, n_heads),
        in_specs=[new, new, new, old, old],
        out_specs=new,
        out_shape=jax.ShapeDtypeStruct(q.shape, BF16),
        scratch_shapes=[pltpu.VMEM((SB_BLOCK, HEAD_DIM), BF16), pltpu.VMEM((SB_BLOCK, HEAD_DIM), BF16)],
        compiler_params=_params("parallel", "parallel"),
    )(q, k_new, v_new, cache_k, cache_v)


def _softmax_pv(s, v):
    m = jnp.max(s, axis=-1, keepdims=True)
    e = jnp.exp(s - m)
    denom = jnp.sum(e, axis=-1, keepdims=True)
    return jnp.dot(e.astype(BF16), v, preferred_element_type=F32) / denom


def _band_prompt_kernel(q_ref, k_ref, v_ref, bias_ref, o_ref, kpad_ref, vpad_ref, *, scale):
    seq = q_ref.shape[0]
    kpad_ref[0:B_REACH, :] = jnp.zeros((B_REACH, HEAD_DIM), BF16)
    vpad_ref[0:B_REACH, :] = jnp.zeros((B_REACH, HEAD_DIM), BF16)
    kpad_ref[B_REACH:, :] = k_ref[...]
    vpad_ref[B_REACH:, :] = v_ref[...]
    bias = bias_ref[...]
    band_idx = lax.broadcasted_iota(jnp.int32, (CHUNK, B_BAND), 1)

    def chunk(c, _):
        start = pl.multiple_of(c * CHUNK, CHUNK)
        q = q_ref[pl.ds(start, CHUNK), :]
        k = kpad_ref[pl.ds(start, B_BAND), :]
        v = vpad_ref[pl.ds(start, B_BAND), :]
        s = lax.dot_general(q, k, (((1,), (1,)), ((), ())), preferred_element_type=F32) * scale + bias
        s = jnp.where(band_idx + (start - B_REACH) >= 0, s, NEG_INF)
        o_ref[pl.ds(start, CHUNK), :] = _softmax_pv(s, v).astype(o_ref.dtype)
        return 0

    lax.fori_loop(0, seq // CHUNK, chunk, 0)


def band_prompt(q, k, v, bias, batch, seq):
    n_heads = q.shape[1] // HEAD_DIM
    assert seq % CHUNK == 0
    spec = pl.BlockSpec((seq, HEAD_DIM), lambda b, h: (b, h))
    return pl.pallas_call(
        functools.partial(_band_prompt_kernel, scale=HEAD_DIM ** -0.5),
        grid=(batch, n_heads),
        in_specs=[spec, spec, spec, pl.BlockSpec((None, CHUNK, B_BAND), lambda b, h: (h, 0, 0))],
        out_specs=spec,
        out_shape=jax.ShapeDtypeStruct(q.shape, BF16),
        scratch_shapes=[pltpu.VMEM((B_REACH + seq, HEAD_DIM), BF16),
                        pltpu.VMEM((B_REACH + seq, HEAD_DIM), BF16)],
        compiler_params=_params("parallel", "parallel"),
    )(q, k, v, bias)


def _band_sample_kernel(q_ref, kn_ref, vn_ref, kc_ref, vc_ref, bias_ref, ok_ref, o_ref, kcat_ref, vcat_ref, *, scale):
    t = q_ref.shape[0]
    b_len = kc_ref.shape[0]
    kcat_ref[...] = jnp.zeros_like(kcat_ref)
    vcat_ref[...] = jnp.zeros_like(vcat_ref)
    kcat_ref[0:b_len, :] = kc_ref[...].astype(BF16)
    vcat_ref[0:b_len, :] = vc_ref[...].astype(BF16)
    kcat_ref[b_len:b_len + t, :] = kn_ref[...]
    vcat_ref[b_len:b_len + t, :] = vn_ref[...]
    s = lax.dot_general(q_ref[...], kcat_ref[...], (((1,), (1,)), ((), ())),
                        preferred_element_type=F32) * scale + bias_ref[...]
    s = jnp.where(ok_ref[...] > 0, s, NEG_INF)
    o_ref[...] = _softmax_pv(s, vcat_ref[...]).astype(o_ref.dtype)


def band_sample(q, k_new, v_new, cache_k, cache_v, bias, ok, batch, t):
    n_heads = q.shape[1] // HEAD_DIM
    b_len = cache_k.shape[1]
    width = bias.shape[2]
    new = pl.BlockSpec((t, HEAD_DIM), lambda b, h: (b, h))
    old = pl.BlockSpec((None, b_len, HEAD_DIM), lambda b, h: (b, 0, h))
    return pl.pallas_call(
        functools.partial(_band_sample_kernel, scale=HEAD_DIM ** -0.5),
        grid=(batch, n_heads),
        in_specs=[new, new, new, old, old,
                  pl.BlockSpec((None, t, width), lambda b, h: (h, 0, 0)),
                  pl.BlockSpec((t, width), lambda b, h: (0, 0))],
        out_specs=new,
        out_shape=jax.ShapeDtypeStruct(q.shape, BF16),
        scratch_shapes=[pltpu.VMEM((width, HEAD_DIM), BF16), pltpu.VMEM((width, HEAD_DIM), BF16)],
        compiler_params=_params("parallel", "parallel"),
    )(q, k_new, v_new, cache_k, cache_v, bias, ok)


def _band_tables(rel_bias, past, b_len, t):
    rel_p = np.clip(np.arange(CHUNK)[:, None] - np.arange(B_BAND)[None, :] + B_REACH, -REL_CLIP, REL_CLIP) + REL_CLIP
    bias_p = rel_bias[:, rel_p]
    width = -(-(b_len + t) // 128) * 128
    q_pos = past + np.arange(t)
    k_pos = np.concatenate([past - b_len + np.arange(b_len), q_pos, np.zeros(width - b_len - t, np.int64)])
    valid = np.arange(width) < b_len + t
    rel_s = np.clip(q_pos[:, None] - k_pos[None, :], -REL_CLIP, REL_CLIP) + REL_CLIP
    qc = q_pos[:, None] // CHUNK
    kc = k_pos[None, :] // CHUNK
    ok = (k_pos[None, :] >= 0) & (kc <= qc) & (kc >= qc - LEFT_CHUNKS) & valid[None, :]
    return bias_p, rel_bias[:, rel_s], jnp.asarray(ok.astype(np.int32))


def kernel(x_prompt, x_sample, cache_a_k, cache_a_v, cache_b_k, cache_b_v, a_w_qkv, a_w_o, kv_w, b_w_q, b_w_o,
           b_rel_bias, ffn_w_in, ffn_w_out, ln_g, ln_b):
    batch, seq, d = x_prompt.shape
    dec_batch, t, _ = x_sample.shape
    n_a = a_w_qkv.shape[0]
    n_b = b_w_q.shape[0]
    depth = n_a + n_b
    past = cache_a_k.shape[2]
    b_len = cache_b_k.shape[1]
    n_heads = d // HEAD_DIM
    alpha = (2 * depth) ** 0.25

    streams = [
        dict(x=x_prompt.reshape(batch * seq, d), nb=batch, n=seq, prompt=True),
        dict(x=x_sample.reshape(dec_batch * t, d), nb=dec_batch, n=t, prompt=False),
    ]
    for st in streams:
        st["xb"] = st["x"].astype(BF16)
        st["a_k"], st["a_v"] = [], []

    cache_a_k = cache_a_k.reshape(n_a, dec_batch, past, d)
    cache_a_v = cache_a_v.reshape(n_a, dec_batch, past, d)
    cache_b_k = cache_b_k.reshape(dec_batch, b_len, d)
    cache_b_v = cache_b_v.reshape(dec_batch, b_len, d)
    kv_wb = kv_w.astype(BF16)

    for l in range(depth):
        w_in = ffn_w_in[l].astype(BF16)
        w_out = ffn_w_out[l].astype(BF16)
        if l < n_a:
            w_qkv = a_w_qkv[l].astype(BF16)
            w_o = a_w_o[l].astype(BF16)
        else:
            j = l - n_a
            w_q = b_w_q[j].astype(BF16)
            w_o = b_w_o[j].astype(BF16)
            bias_p, bias_s, ok_s = _band_tables(b_rel_bias[j], past, b_len, t)
        for st in streams:
            xb = st["xb"]
            if l < n_a:
                (qb,) = matmul(xb, w_qkv, [BF16], col0=0, ncols=d)
                k32, kb = matmul(xb, w_qkv, [F32, BF16], col0=d, ncols=d)
                v32, vb = matmul(xb, w_qkv, [F32, BF16], col0=2 * d, ncols=d)
                st["a_k"].append(k32.reshape(st["nb"], st["n"], n_heads, HEAD_DIM))
                st["a_v"].append(v32.reshape(st["nb"], st["n"], n_heads, HEAD_DIM))
                if st["prompt"]:
                    mix = stick_breaking_prompt(qb, kb, vb, batch, seq)
                else:
                    mix = stick_breaking_sample(qb, kb, vb, cache_a_k[l], cache_a_v[l], dec_batch, t)
            else:
                if l == n_a:
                    st["bk32"], st["bkb"] = matmul(xb, kv_wb, [F32, BF16], col0=0, ncols=d)
                    st["bv32"], st["bvb"] = matmul(xb, kv_wb, [F32, BF16], col0=d, ncols=d)
                (qb,) = matmul(xb, w_q, [BF16])
                if st["prompt"]:
                    mix = band_prompt(qb, st["bkb"], st["bvb"], bias_p, batch, seq)
                else:
                    mix = band_sample(qb, st["bkb"], st["bvb"], cache_b_k, cache_b_v, bias_s, ok_s, dec_batch, t)
            (branch,) = matmul(mix, w_o, [F32])
            st["x"], st["xb"] = residual_ln(st["x"], branch, ln_g[l, 0], ln_b[l, 0], alpha)
            hidden = swiglu_in(st["xb"], w_in)
            branch = matmul_ksplit(hidden, w_out)
            st["x"], st["xb"] = residual_ln(st["x"], branch, ln_g[l, 1], ln_b[l, 1], alpha)

    sp, ss = streams
    keep = min(B_REACH, seq)

    def heads(a, st):
        return a.reshape(st["nb"], st["n"], n_heads, HEAD_DIM)

    return (sp["x"].reshape(batch, seq, d), ss["x"].reshape(dec_batch, t, d),
            jnp.stack(sp["a_k"]), jnp.stack(sp["a_v"]),
            heads(sp["bk32"], sp)[:, seq - keep:], heads(sp["bv32"], sp)[:, seq - keep:],
            jnp.stack(ss["a_k"]), jnp.stack(ss["a_v"]),
            heads(ss["bk32"], ss), heads(ss["bv32"], ss))
```

```python
import functools

import numpy as np
import jax
import jax.numpy as jnp
from jax import lax
from jax.experimental import pallas as pl
from jax.experimental.pallas import tpu as pltpu

HEAD_DIM = 128
CHUNK = 64
LEFT_CHUNKS = 8
B_REACH = LEFT_CHUNKS * CHUNK
B_BAND = B_REACH + CHUNK
REL_CLIP = 128
LN_EPS = 1e-5
NEG_INF = -1e30
SB_BLOCK = 128
SB_GROUP = 4
SB_HEADS = 2
BAND_UNROLL = 8
HEAD_TILE = 8
V7X_VMEM_BYTES = 64 * 1024 * 1024
VMEM_LIMIT = V7X_VMEM_BYTES - 8 * 1024 * 1024

F32 = jnp.float32
BF16 = jnp.bfloat16
NT_DIMS = (((1,), (1,)), ((), ()))


def _params(*sem):
    return pltpu.CompilerParams(dimension_semantics=sem, vmem_limit_bytes=VMEM_LIMIT)


def _mm_kernel(x_ref, w_ref, *o_refs):
    acc = jnp.dot(x_ref[...], w_ref[...], preferred_element_type=F32)
    for o_ref in o_refs:
        o_ref[...] = acc.astype(o_ref.dtype)


def matmul(x, w, out_dtypes, *, col0=0, ncols=None, bm=1024, bn=1024):
    m, k = x.shape
    ncols = w.shape[1] if ncols is None else ncols
    bm = min(bm, m)
    bn = min(bn, ncols)
    assert m % bm == 0 and ncols % bn == 0 and col0 % bn == 0
    off = col0 // bn
    outs = pl.pallas_call(
        _mm_kernel,
        grid=(m // bm, ncols // bn),
        in_specs=[pl.BlockSpec((bm, k), lambda i, j: (i, 0)),
                  pl.BlockSpec((k, bn), lambda i, j: (0, j + off))],
        out_specs=[pl.BlockSpec((bm, bn), lambda i, j: (i, j)) for _ in out_dtypes],
        out_shape=[jax.ShapeDtypeStruct((m, ncols), dt) for dt in out_dtypes],
        compiler_params=_params("parallel", "parallel"),
    )(x, w)
    return outs


def _mm_acc_kernel(x_ref, w_ref, o_ref, acc_ref):
    kk = pl.program_id(2)
    part = jnp.dot(x_ref[...], w_ref[...], preferred_element_type=F32)

    @pl.when(kk == 0)
    def _():
        acc_ref[...] = part

    @pl.when(jnp.logical_and(kk > 0, kk < pl.num_programs(2) - 1))
    def _():
        acc_ref[...] += part

    @pl.when(kk == pl.num_programs(2) - 1)
    def _():
        o_ref[...] = acc_ref[...] + part


def matmul_ksplit(x, w, *, bm=1024, bn=512, nk=2):
    m, k = x.shape
    n = w.shape[1]
    bm = min(bm, m)
    bn = min(bn, n)
    bk = k // nk
    assert m % bm == 0 and n % bn == 0 and k % nk == 0 and bk % 128 == 0 and nk >= 2
    return pl.pallas_call(
        _mm_acc_kernel,
        grid=(m // bm, n // bn, nk),
        in_specs=[pl.BlockSpec((bm, bk), lambda i, j, kk: (i, kk)),
                  pl.BlockSpec((bk, bn), lambda i, j, kk: (kk, j))],
        out_specs=pl.BlockSpec((bm, bn), lambda i, j, kk: (i, j)),
        out_shape=jax.ShapeDtypeStruct((m, n), F32),
        scratch_shapes=[pltpu.VMEM((bm, bn), F32)],
        compiler_params=_params("parallel", "parallel", "arbitrary"),
    )(x, w)


def _swiglu_kernel(x_ref, wg_ref, wu_ref, o_ref):
    x = x_ref[...]
    gate = jnp.dot(x, wg_ref[...], preferred_element_type=F32)
    up = jnp.dot(x, wu_ref[...], preferred_element_type=F32)
    o_ref[...] = (gate / (1.0 + jnp.exp(-gate)) * up).astype(o_ref.dtype)


def swiglu_in(x, w_in, *, bm=1024, bn=256):
    m, k = x.shape
    f = w_in.shape[1] // 2
    bm = min(bm, m)
    assert m % bm == 0 and f % bn == 0
    nj = f // bn
    return pl.pallas_call(
        _swiglu_kernel,
        grid=(m // bm, nj),
        in_specs=[pl.BlockSpec((bm, k), lambda i, j: (i, 0)),
                  pl.BlockSpec((k, bn), lambda i, j: (0, j)),
                  pl.BlockSpec((k, bn), lambda i, j: (0, j + nj))],
        out_specs=pl.BlockSpec((bm, bn), lambda i, j: (i, j)),
        out_shape=jax.ShapeDtypeStruct((m, f), BF16),
        compiler_params=_params("parallel", "parallel"),
    )(x, w_in, w_in)


def _ln_kernel(x_ref, m_ref, g_ref, b_ref, o_ref, ob_ref, *, alpha):
    y = alpha * x_ref[...] + m_ref[...]
    mu = jnp.mean(y, axis=-1, keepdims=True)
    yc = y - mu
    var = jnp.mean(yc * yc, axis=-1, keepdims=True)
    out = yc * lax.rsqrt(var + LN_EPS) * g_ref[...] + b_ref[...]
    o_ref[...] = out
    ob_ref[...] = out.astype(ob_ref.dtype)


def residual_ln(x, branch, g, b, alpha, *, bm=256):
    m, d = x.shape
    bm = min(bm, m)
    assert m % bm == 0
    row = pl.BlockSpec((bm, d), lambda i: (i, 0))
    vec = pl.BlockSpec((1, d), lambda i: (0, 0))
    return pl.pallas_call(
        functools.partial(_ln_kernel, alpha=alpha),
        grid=(m // bm,),
        in_specs=[row, row, vec, vec],
        out_specs=[row, row],
        out_shape=[jax.ShapeDtypeStruct((m, d), F32), jax.ShapeDtypeStruct((m, d), BF16)],
        compiler_params=_params("parallel"),
    )(x, branch, g.reshape(1, d), b.reshape(1, d))


def _suffix_sum_matrix():
    blk = SB_BLOCK
    r = lax.broadcasted_iota(jnp.int32, (2 * blk, 2 * blk), 0) % blk
    c = lax.broadcasted_iota(jnp.int32, (2 * blk, 2 * blk), 1)
    return jnp.where(jnp.logical_or(r > c, c >= blk), 1.0, 0.0).astype(BF16)


def _rows_concat(parts):
    return parts[0] if len(parts) == 1 else jnp.concatenate(parts, axis=0)


def _mask_top_rows(x, mask):
    rb = mask.shape[0]
    top = jnp.where(mask, x[:rb], 0.0)
    return top if x.shape[0] == rb else jnp.concatenate([top, x[rb:]], axis=0)


def _sb_tile(q, k, v, carry, suffix, diag_causal, scale):
    (result,) = _sb_tiles([(q, k, v, carry)], suffix, diag_causal, scale)
    return result


def _sb_tiles(streams, suffix, diag_causal, scale):
    blk = SB_BLOCK
    rows = streams[0][0].shape[0]
    nw = streams[0][1].shape[0] // blk
    diag = diag_causal is not None
    rb = rows // nw if diag else rows
    first_row = [j * rb if diag else 0 for j in range(nw)]
    starts = np.cumsum([0] + [rows - r0 for r0 in first_row])

    scores = [lax.dot_general(q, k, NT_DIMS, preferred_element_type=F32) for q, k, _, _ in streams]

    log_betas, splits = [], []
    for z in scores:
        lbs, pieces = [], []
        for j, r0 in enumerate(first_row):
            zj = z[r0:, j * blk:(j + 1) * blk] * scale
            log_beta = jnp.minimum(zj, 0.0) - jnp.log(1.0 + jnp.exp(-jnp.abs(zj)))
            log_keep = log_beta - zj
            if diag:
                log_keep = _mask_top_rows(log_keep, diag_causal)
            lbs.append(log_beta)
            pieces.append(log_keep)
        stacked = _rows_concat(pieces)
        hi = stacked.astype(BF16)
        lo = (stacked - hi.astype(F32)).astype(BF16)
        log_betas.append(lbs)
        splits.append(jnp.concatenate([hi, lo], axis=1))

    suffix_sums = [jnp.dot(hl, suffix, preferred_element_type=F32) for hl in splits]

    carries, weights = [], []
    for (_, _, _, carry), lbs, sums in zip(streams, log_betas, suffix_sums):
        cols = [None] * nw
        for j in reversed(range(nw)):
            r0 = first_row[j]
            after = sums[starts[j]:starts[j + 1], :blk]
            total = sums[starts[j]:starts[j + 1], blk:]
            w = jnp.exp(lbs[j] + after + carry[r0:])
            if diag:
                w = _mask_top_rows(w, diag_causal)
            if r0 == 0:
                cols[j] = w.astype(BF16)
                carry = carry + total
            else:
                cols[j] = jnp.concatenate([jnp.zeros((r0, blk), BF16), w.astype(BF16)], axis=0)
                carry = jnp.concatenate([carry[:r0], carry[r0:] + total], axis=0)
        carries.append(carry)
        weights.append(jnp.concatenate(cols, axis=1) if nw > 1 else cols[0])

    return [(carry, jnp.dot(w, v, preferred_element_type=F32))
            for carry, w, (_, _, v, _) in zip(carries, weights, streams)]


def _sb_prompt_kernel(q_ref, k_ref, v_ref, o_ref, carry_ref, acc_ref, *, scale):
    blk = SB_BLOCK
    grp = SB_GROUP * blk
    suffix = _suffix_sum_matrix()
    row = lax.broadcasted_iota(jnp.int32, (blk, blk), 0)
    col = lax.broadcasted_iota(jnp.int32, (blk, blk), 1)
    diag_causal = col < row
    head_lanes = [slice(hh * HEAD_DIM, (hh + 1) * HEAD_DIM) for hh in range(SB_HEADS)]

    def q_group(g, _):
        g0 = pl.multiple_of(g * grp, grp)
        zero = jnp.zeros((grp, HEAD_DIM), F32)
        streams = [(q_ref[pl.ds(g0, grp), lanes], k_ref[pl.ds(g0, grp), lanes], v_ref[pl.ds(g0, grp), lanes], zero)
                   for lanes in head_lanes]
        for hh, (carry, out) in enumerate(_sb_tiles(streams, suffix, diag_causal, scale)):
            carry_ref[hh] = carry
            acc_ref[hh] = out

        def k_group(t, _):
            k0 = pl.multiple_of((g - 1 - t) * grp, grp)
            streams = [(q_ref[pl.ds(g0, grp), lanes], k_ref[pl.ds(k0, grp), lanes], v_ref[pl.ds(k0, grp), lanes],
                        carry_ref[hh]) for hh, lanes in enumerate(head_lanes)]
            for hh, (carry, out) in enumerate(_sb_tiles(streams, suffix, None, scale)):
                carry_ref[hh] = carry
                acc_ref[hh] += out
            return 0

        lax.fori_loop(0, g, k_group, 0)
        for hh, lanes in enumerate(head_lanes):
            o_ref[pl.ds(g0, grp), lanes] = acc_ref[hh].astype(o_ref.dtype)
        return 0

    lax.fori_loop(0, q_ref.shape[0] // grp, q_group, 0)


def stick_breaking_prompt(q, k, v, batch, seq):
    n_heads = q.shape[1] // HEAD_DIM
    grp = SB_GROUP * SB_BLOCK
    assert seq % grp == 0 and n_heads % SB_HEADS == 0
    spec = pl.BlockSpec((seq, SB_HEADS * HEAD_DIM), lambda b, h: (b, h))
    return pl.pallas_call(
        functools.partial(_sb_prompt_kernel, scale=HEAD_DIM ** -0.5),
        grid=(batch, n_heads // SB_HEADS),
        in_specs=[spec, spec, spec],
        out_specs=spec,
        out_shape=jax.ShapeDtypeStruct(q.shape, BF16),
        scratch_shapes=[pltpu.VMEM((SB_HEADS, grp, HEAD_DIM), F32), pltpu.VMEM((SB_HEADS, grp, HEAD_DIM), F32)],
        compiler_params=_params("parallel", "parallel"),
    )(q, k, v)


def _head_rows(cache_ref, hh):
    rows = cache_ref.shape[0]
    flat = cache_ref.reshape(rows * HEAD_TILE, HEAD_DIM)
    return flat[pl.ds(hh, rows, stride=HEAD_TILE), :].astype(BF16)


def _sb_sample_kernel(q_ref, kn_ref, vn_ref, kc_ref, vc_ref, o_ref, kpad_ref, vpad_ref, *, scale):
    blk = SB_BLOCK
    t = q_ref.shape[0]
    suffix = _suffix_sum_matrix()
    row = lax.broadcasted_iota(jnp.int32, (t, blk), 0)
    col = lax.broadcasted_iota(jnp.int32, (t, blk), 1)
    kpad_ref[...] = jnp.zeros_like(kpad_ref)
    vpad_ref[...] = jnp.zeros_like(vpad_ref)
    for hh in range(HEAD_TILE):
        lanes = slice(hh * HEAD_DIM, (hh + 1) * HEAD_DIM)
        q = q_ref[:, lanes]
        kpad_ref[0:t, :] = kn_ref[:, lanes]
        vpad_ref[0:t, :] = vn_ref[:, lanes]
        carry, out_new = _sb_tile(q, kpad_ref[...], vpad_ref[...], jnp.zeros((t, HEAD_DIM), F32), suffix,
                                  col < row, scale)
        _, out_old = _sb_tile(q, _head_rows(kc_ref, hh), _head_rows(vc_ref, hh), carry, suffix, None, scale)
        o_ref[:, lanes] = (out_new + out_old).astype(o_ref.dtype)


def stick_breaking_sample(q, k_new, v_new, cache_k, cache_v, layer, batch, t):
    n_heads = q.shape[1] // HEAD_DIM
    past = cache_k.shape[2]
    assert past % SB_BLOCK == 0 and t <= SB_BLOCK and t % 16 == 0 and n_heads % HEAD_TILE == 0
    new = pl.BlockSpec((t, HEAD_TILE * HEAD_DIM), lambda b, h: (b, h))
    old = pl.BlockSpec((None, None, past, HEAD_TILE, HEAD_DIM), lambda b, h: (layer, b, 0, h, 0))
    return pl.pallas_call(
        functools.partial(_sb_sample_kernel, scale=HEAD_DIM ** -0.5),
        grid=(batch, n_heads // HEAD_TILE),
        in_specs=[new, new, new, old, old],
        out_specs=new,
        out_shape=jax.ShapeDtypeStruct(q.shape, BF16),
        scratch_shapes=[pltpu.VMEM((SB_BLOCK, HEAD_DIM), BF16), pltpu.VMEM((SB_BLOCK, HEAD_DIM), BF16)],
        compiler_params=_params("parallel", "parallel"),
    )(q, k_new, v_new, cache_k, cache_v)


def _softmax_pv(s, v):
    m = jnp.max(s, axis=-1, keepdims=True)
    e = jnp.exp(s - m)
    denom = jnp.sum(e, axis=-1, keepdims=True)
    return jnp.dot(e.astype(BF16), v, preferred_element_type=F32) / denom


def _band_prompt_kernel(q_ref, k_ref, v_ref, bias_ref, o_ref, kpad_ref, vpad_ref, *, scale):
    seq = q_ref.shape[0]
    kpad_ref[0:B_REACH, :] = jnp.zeros((B_REACH, HEAD_DIM), BF16)
    vpad_ref[0:B_REACH, :] = jnp.zeros((B_REACH, HEAD_DIM), BF16)
    kpad_ref[B_REACH:, :] = k_ref[...]
    vpad_ref[B_REACH:, :] = v_ref[...]
    band_idx = lax.broadcasted_iota(jnp.int32, (CHUNK, B_BAND), 1)

    def chunks(i, _):
        starts = [pl.multiple_of((i * BAND_UNROLL + u) * CHUNK, CHUNK) for u in range(BAND_UNROLL)]
        scores = [lax.dot_general(q_ref[pl.ds(st, CHUNK), :], kpad_ref[pl.ds(st, B_BAND), :], NT_DIMS,
                                  preferred_element_type=F32) for st in starts]
        probs, denoms = [], []
        for st, s in zip(starts, scores):
            s = s * scale + bias_ref[...]
            s = jnp.where(band_idx + (st - B_REACH) >= 0, s, NEG_INF)
            e = jnp.exp(s - jnp.max(s, axis=-1, keepdims=True))
            probs.append(e.astype(BF16))
            denoms.append(jnp.sum(e, axis=-1, keepdims=True))
        outs = [jnp.dot(p, vpad_ref[pl.ds(st, B_BAND), :], preferred_element_type=F32) for st, p in zip(starts, probs)]
        for st, out, denom in zip(starts, outs, denoms):
            o_ref[pl.ds(st, CHUNK), :] = (out / denom).astype(o_ref.dtype)
        return 0

    lax.fori_loop(0, seq // (CHUNK * BAND_UNROLL), chunks, 0)


def band_prompt(q, k, v, bias, batch, seq):
    n_heads = q.shape[1] // HEAD_DIM
    assert seq % (CHUNK * BAND_UNROLL) == 0
    spec = pl.BlockSpec((seq, HEAD_DIM), lambda b, h: (b, h))
    return pl.pallas_call(
        functools.partial(_band_prompt_kernel, scale=HEAD_DIM ** -0.5),
        grid=(batch, n_heads),
        in_specs=[spec, spec, spec, pl.BlockSpec((None, CHUNK, B_BAND), lambda b, h: (h, 0, 0))],
        out_specs=spec,
        out_shape=jax.ShapeDtypeStruct(q.shape, BF16),
        scratch_shapes=[pltpu.VMEM((B_REACH + seq, HEAD_DIM), BF16),
                        pltpu.VMEM((B_REACH + seq, HEAD_DIM), BF16)],
        compiler_params=_params("parallel", "parallel"),
    )(q, k, v, bias)


def _band_sample_kernel(q_ref, kn_ref, vn_ref, kc_ref, vc_ref, bias_ref, ok_ref, o_ref, kcat_ref, vcat_ref, *, scale):
    t = q_ref.shape[0]
    b_len = kc_ref.shape[0]
    kcat_ref[...] = jnp.zeros_like(kcat_ref)
    vcat_ref[...] = jnp.zeros_like(vcat_ref)
    for hh in range(HEAD_TILE):
        lanes = slice(hh * HEAD_DIM, (hh + 1) * HEAD_DIM)
        kcat_ref[0:b_len, :] = _head_rows(kc_ref, hh)
        vcat_ref[0:b_len, :] = _head_rows(vc_ref, hh)
        kcat_ref[b_len:b_len + t, :] = kn_ref[:, lanes]
        vcat_ref[b_len:b_len + t, :] = vn_ref[:, lanes]
        s = lax.dot_general(q_ref[:, lanes], kcat_ref[...], NT_DIMS, preferred_element_type=F32) * scale + bias_ref[hh]
        s = jnp.where(ok_ref[...] > 0, s, NEG_INF)
        o_ref[:, lanes] = _softmax_pv(s, vcat_ref[...]).astype(o_ref.dtype)


def band_sample(q, k_new, v_new, cache_k, cache_v, bias, ok, batch, t):
    n_heads = q.shape[1] // HEAD_DIM
    b_len = cache_k.shape[1]
    width = bias.shape[2]
    assert n_heads % HEAD_TILE == 0
    new = pl.BlockSpec((t, HEAD_TILE * HEAD_DIM), lambda b, h: (b, h))
    old = pl.BlockSpec((None, b_len, HEAD_TILE, HEAD_DIM), lambda b, h: (b, 0, h, 0))
    return pl.pallas_call(
        functools.partial(_band_sample_kernel, scale=HEAD_DIM ** -0.5),
        grid=(batch, n_heads // HEAD_TILE),
        in_specs=[new, new, new, old, old,
                  pl.BlockSpec((HEAD_TILE, t, width), lambda b, h: (h, 0, 0)),
                  pl.BlockSpec((t, width), lambda b, h: (0, 0))],
        out_specs=new,
        out_shape=jax.ShapeDtypeStruct(q.shape, BF16),
        scratch_shapes=[pltpu.VMEM((width, HEAD_DIM), BF16), pltpu.VMEM((width, HEAD_DIM), BF16)],
        compiler_params=_params("parallel", "parallel"),
    )(q, k_new, v_new, cache_k, cache_v, bias, ok)


def _band_tables(rel_bias, past, b_len, t):
    rel_p = np.clip(np.arange(CHUNK)[:, None] - np.arange(B_BAND)[None, :] + B_REACH, -REL_CLIP, REL_CLIP) + REL_CLIP
    bias_p = rel_bias[:, rel_p]
    width = -(-(b_len + t) // 128) * 128
    q_pos = past + np.arange(t)
    k_pos = np.concatenate([past - b_len + np.arange(b_len), q_pos, np.zeros(width - b_len - t, np.int64)])
    valid = np.arange(width) < b_len + t
    rel_s = np.clip(q_pos[:, None] - k_pos[None, :], -REL_CLIP, REL_CLIP) + REL_CLIP
    qc = q_pos[:, None] // CHUNK
    kc = k_pos[None, :] // CHUNK
    ok = (k_pos[None, :] >= 0) & (kc <= qc) & (kc >= qc - LEFT_CHUNKS) & valid[None, :]
    return bias_p, rel_bias[:, rel_s], jnp.asarray(ok.astype(np.int32))


def kernel(x_prompt, x_sample, cache_a_k, cache_a_v, cache_b_k, cache_b_v, a_w_qkv, a_w_o, kv_w, b_w_q, b_w_o,
           b_rel_bias, ffn_w_in, ffn_w_out, ln_g, ln_b):
    batch, seq, d = x_prompt.shape
    dec_batch, t, _ = x_sample.shape
    n_a = a_w_qkv.shape[0]
    n_b = b_w_q.shape[0]
    depth = n_a + n_b
    past = cache_a_k.shape[2]
    b_len = cache_b_k.shape[1]
    n_heads = d // HEAD_DIM
    alpha = (2 * depth) ** 0.25

    streams = [
        dict(x=x_prompt.reshape(batch * seq, d), nb=batch, n=seq, prompt=True),
        dict(x=x_sample.reshape(dec_batch * t, d), nb=dec_batch, n=t, prompt=False),
    ]
    for st in streams:
        st["xb"] = st["x"].astype(BF16)
        st["a_k"], st["a_v"] = [], []

    kv_wb = kv_w.astype(BF16)

    for l in range(depth):
        w_in = ffn_w_in[l].astype(BF16)
        w_out = ffn_w_out[l].astype(BF16)
        if l < n_a:
            w_qkv = a_w_qkv[l].astype(BF16)
            w_o = a_w_o[l].astype(BF16)
        else:
            j = l - n_a
            w_q = b_w_q[j].astype(BF16)
            w_o = b_w_o[j].astype(BF16)
            bias_p, bias_s, ok_s = _band_tables(b_rel_bias[j], past, b_len, t)
        for st in streams:
            xb = st["xb"]
            if l < n_a:
                (qb,) = matmul(xb, w_qkv, [BF16], col0=0, ncols=d)
                k32, kb = matmul(xb, w_qkv, [F32, BF16], col0=d, ncols=d)
                v32, vb = matmul(xb, w_qkv, [F32, BF16], col0=2 * d, ncols=d)
                st["a_k"].append(k32.reshape(st["nb"], st["n"], n_heads, HEAD_DIM))
                st["a_v"].append(v32.reshape(st["nb"], st["n"], n_heads, HEAD_DIM))
                if st["prompt"]:
                    mix = stick_breaking_prompt(qb, kb, vb, batch, seq)
                else:
                    mix = stick_breaking_sample(qb, kb, vb, cache_a_k, cache_a_v, l, dec_batch, t)
            else:
                if l == n_a:
                    st["bk32"], st["bkb"] = matmul(xb, kv_wb, [F32, BF16], col0=0, ncols=d)
                    st["bv32"], st["bvb"] = matmul(xb, kv_wb, [F32, BF16], col0=d, ncols=d)
                (qb,) = matmul(xb, w_q, [BF16])
                if st["prompt"]:
                    mix = band_prompt(qb, st["bkb"], st["bvb"], bias_p, batch, seq)
                else:
                    mix = band_sample(qb, st["bkb"], st["bvb"], cache_b_k, cache_b_v, bias_s, ok_s, dec_batch, t)
            (branch,) = matmul(mix, w_o, [F32])
            st["x"], st["xb"] = residual_ln(st["x"], branch, ln_g[l, 0], ln_b[l, 0], alpha)
            hidden = swiglu_in(st["xb"], w_in)
            branch = matmul_ksplit(hidden, w_out)
            st["x"], st["xb"] = residual_ln(st["x"], branch, ln_g[l, 1], ln_b[l, 1], alpha)

    sp, ss = streams
    keep = min(B_REACH, seq)

    def heads(a, st):
        return a.reshape(st["nb"], st["n"], n_heads, HEAD_DIM)

    return (sp["x"].reshape(batch, seq, d), ss["x"].reshape(dec_batch, t, d),
            jnp.stack(sp["a_k"]), jnp.stack(sp["a_v"]),
            heads(sp["bk32"], sp)[:, seq - keep:], heads(sp["bv32"], sp)[:, seq - keep:],
            jnp.stack(ss["a_k"]), jnp.stack(ss["a_v"]),
            heads(ss["bk32"], ss), heads(ss["bv32"], ss))
```
